```python
import jax
import jax.numpy as jnp
from jax import lax
import numpy as np

D_MODEL = 2048
BATCH = 8
SEQ = 2048
DEPTH = 2

GRID_W = 64
CTX_LEN = 256
HEAD_DIM = 128
N_Q_HEADS = 16
N_KV_HEADS = 4
Q_BLOCK = 128
ROPE_THETA = 10000.0
POOL_GROUPS = 4
POOL_GROUP_DIM = 256
POOL_WINDOWS = (2, 4, 8, 16)
POOL_OUT_DIM = D_MODEL // POOL_GROUPS
PEER_HEADS = 8
PEER_N_KEYS = 128
PEER_N_EXPERTS = PEER_N_KEYS * PEER_N_KEYS
PEER_QUERY_DIM = 256
PEER_TOPK = 16
PEER_TOKEN_BLOCK = 128
EPS = 1e-6

Q_W = N_Q_HEADS * HEAD_DIM
KV_W = N_KV_HEADS * HEAD_DIM
POOL_W = POOL_GROUPS * POOL_GROUP_DIM
KV_OFF = Q_W
POOL_OFF = Q_W + 2 * KV_W
GA_OFF = POOL_OFF + POOL_W
GB_OFF = GA_OFF + D_MODEL
IN_W = GB_OFF + D_MODEL

kernel_name = 'hybrid_gqa_pool_peer_dit_block'


def rms(x):
    xf = x.astype(jnp.float32)
    return (xf * lax.rsqrt(jnp.mean(xf * xf, axis=-1, keepdims=True) + EPS)).astype(x.dtype)


def modulate(x, shift, scale):
    return rms(x) * (1 + scale) + shift


def rope_tables(L, dtype):
    rows = L // GRID_W
    row = jnp.repeat(jnp.arange(rows), GRID_W)
    col = jnp.tile(jnp.arange(GRID_W), rows)
    n_freq = HEAD_DIM // 4
    inv = ROPE_THETA ** (-jnp.arange(n_freq, dtype=jnp.float32) / n_freq)
    ang_r = row.astype(jnp.float32)[:, None, None] * inv
    ang_c = col.astype(jnp.float32)[:, None, None] * inv
    return (jnp.cos(ang_r).astype(dtype), jnp.sin(ang_r).astype(dtype),
            jnp.cos(ang_c).astype(dtype), jnp.sin(ang_c).astype(dtype))


def rotate(x, cos, sin):
    half = x.shape[-1] // 2
    x1, x2 = x[..., :half], x[..., half:]
    return jnp.concatenate([x1 * cos - x2 * sin, x2 * cos + x1 * sin], axis=-1)


def rope_2d(x, tabs):
    cos_r, sin_r, cos_c, sin_c = tabs
    half = HEAD_DIM // 2
    return jnp.concatenate([rotate(x[..., :half], cos_r, sin_r),
                            rotate(x[..., half:], cos_c, sin_c)], axis=-1)


def split_kv(kv, k_gain):
    B, L, _ = kv.shape
    k = rms(kv[..., :KV_W].reshape(B, L, N_KV_HEADS, HEAD_DIM)) * k_gain
    v = kv[..., KV_W:].reshape(B, L, N_KV_HEADS, HEAD_DIM)
    return k, v


def attn_block(qb, k, v):
    s = jnp.einsum('bqhgd,bkhd->bhgqk', qb, k).astype(jnp.float32) * (HEAD_DIM ** -0.5)
    p = jax.nn.softmax(s, axis=-1).astype(v.dtype)
    return jnp.einsum('bhgqk,bkhd->bqhgd', p, v)


def blocked_attention(q, k, v):
    B, L, Hq, Dh = q.shape
    G = Hq // N_KV_HEADS
    nb = L // Q_BLOCK
    qb = q.reshape(B, nb, Q_BLOCK, N_KV_HEADS, G, Dh).transpose(1, 0, 2, 3, 4, 5)
    o = lax.map(lambda blk: attn_block(blk, k, v), qb)
    return o.transpose(1, 0, 2, 3, 4, 5).reshape(B, L, Hq * Dh)


def multiscale_pool(u):
    B, L, _ = u.shape
    ug = u.reshape(B, L, POOL_GROUPS, POOL_GROUP_DIM)
    csum = jnp.concatenate([jnp.zeros((B, 1, POOL_GROUPS, POOL_GROUP_DIM), jnp.float32),
                            jnp.cumsum(ug.astype(jnp.float32), axis=1)], axis=1)
    t = jnp.arange(L)[:, None]
    w = jnp.array(POOL_WINDOWS)[None, :]
    lo = jnp.clip(t - w // 2, 0, L)
    hi = jnp.clip(t + (w - w // 2), 0, L)
    g_idx = jnp.arange(POOL_GROUPS)[None, :]
    sums = csum[:, hi, g_idx, :] - csum[:, lo, g_idx, :]
    count = (hi - lo).astype(jnp.float32)[None, :, :, None]
    return (sums / count).astype(u.dtype) - ug


def mix_tokens(p, k, v, q_gain, w_br_attn, w_pool, pool_scale, w_out, tabs):
    B, L, _ = p.shape
    q = rms(p[..., :Q_W].reshape(B, L, N_Q_HEADS, HEAD_DIM)) * q_gain
    if tabs is not None:
        q = rope_2d(q, tabs)
    y_a = blocked_attention(q, k, v) @ w_br_attn
    pooled = multiscale_pool(p[..., POOL_OFF:GA_OFF])
    y_b = jnp.einsum('blgc,gcd->blgd', pooled, w_pool).reshape(B, L, D_MODEL) * pool_scale
    merged = (jax.nn.sigmoid(p[..., GA_OFF:GB_OFF]) * y_a
              + jax.nn.sigmoid(p[..., GB_OFF:]) * y_b)
    return merged @ w_out


def peer_ffn(h, w_q_peer, peer_keys, peer_u, peer_v):
    B, L, D = h.shape
    q = (h @ w_q_peer).reshape(B, L, PEER_HEADS, 2, PEER_QUERY_DIM // 2)
    s = jnp.einsum('blhpd,hpkd->blhpk', q, peer_keys).astype(jnp.float32)
    sv, si = lax.top_k(s, PEER_TOPK)
    cand_s = (sv[..., 0, :, None] + sv[..., 1, None, :]).reshape(B, L, PEER_HEADS, PEER_TOPK * PEER_TOPK)
    cand_i = (si[..., 0, :, None] * PEER_N_KEYS + si[..., 1, None, :]).reshape(B, L, PEER_HEADS, PEER_TOPK * PEER_TOPK)
    top_s, pos = lax.top_k(cand_s, PEER_TOPK)
    idx = jnp.take_along_axis(cand_i, pos, axis=-1)
    g = jax.nn.softmax(top_s, axis=-1).astype(h.dtype)
    n_tok = B * L
    nb = n_tok // PEER_TOKEN_BLOCK
    n_sel = PEER_HEADS * PEER_TOPK
    hb = h.reshape(nb, PEER_TOKEN_BLOCK, D)
    ib = idx.reshape(nb, PEER_TOKEN_BLOCK, n_sel)
    gb = g.reshape(nb, PEER_TOKEN_BLOCK, n_sel)

    def token_block(args):
        hc, ic, gc = args
        act = jax.nn.gelu(jnp.einsum('td,ted->te', hc, peer_u[ic]), approximate=False)
        return jnp.einsum('te,ted->td', gc * act, peer_v[ic])

    out = lax.map(token_block, (hb, ib, gb))
    return out.reshape(B, L, D)


def setup_inputs(seed: int = 0) -> dict:
    key = jax.random.key(seed)
    ks = jax.random.split(key, 18)

    def nrm(k, shape, scale):
        return jax.random.normal(k, shape, jnp.float32) * scale

    return {
        'x': nrm(ks[0], (BATCH, SEQ, D_MODEL), 1.0),
        'c': nrm(ks[1], (BATCH, D_MODEL), 1.0),
        'ctx': nrm(ks[2], (BATCH, CTX_LEN, D_MODEL), 1.0),
        'c_ctx': nrm(ks[3], (D_MODEL,), 1.0),
        'w_ada': nrm(ks[4], (DEPTH, D_MODEL, 6 * D_MODEL), 0.5 * D_MODEL ** -0.5),
        'b_ada': nrm(ks[5], (DEPTH, 6 * D_MODEL), 0.01),
        'w_in': nrm(ks[6], (DEPTH, D_MODEL, IN_W), D_MODEL ** -0.5),
        'q_gain': 1.0 + nrm(ks[7], (DEPTH, HEAD_DIM), 0.1),
        'k_gain': 1.0 + nrm(ks[8], (DEPTH, HEAD_DIM), 0.1),
        'w_br_attn': nrm(ks[9], (DEPTH, Q_W, D_MODEL), Q_W ** -0.5),
        'w_pool': nrm(ks[10], (DEPTH, POOL_GROUPS, POOL_GROUP_DIM, POOL_OUT_DIM), POOL_GROUP_DIM ** -0.5),
        'pool_scale': 1.0 + nrm(ks[11], (DEPTH, D_MODEL), 0.1),
        'w_out': nrm(ks[12], (DEPTH, D_MODEL, D_MODEL), D_MODEL ** -0.5),
        'w_q_peer': nrm(ks[13], (DEPTH, D_MODEL, PEER_HEADS * PEER_QUERY_DIM), D_MODEL ** -0.5),
        'peer_keys': nrm(ks[14], (DEPTH, PEER_HEADS, 2, PEER_N_KEYS, PEER_QUERY_DIM // 2), (PEER_QUERY_DIM // 2) ** -0.5),
        'peer_u': nrm(ks[15], (DEPTH, PEER_N_EXPERTS, D_MODEL), D_MODEL ** -0.5),
        'peer_v': nrm(ks[16], (DEPTH, PEER_N_EXPERTS, D_MODEL), 2.0 * (PEER_HEADS * PEER_TOPK) ** -0.5),
        'final_gain': 1.0 + nrm(ks[17], (D_MODEL,), 0.1),
    }


def reference(x, c, ctx, c_ctx, w_ada, b_ada, w_in, q_gain, k_gain, w_br_attn, w_pool,
              pool_scale, w_out, w_q_peer, peer_keys, peer_u, peer_v, final_gain):
    B, L, D = x.shape
    tabs = rope_tables(L, x.dtype)
    for i in range(DEPTH):
        last = i == DEPTH - 1
        mod = jax.nn.silu(c) @ w_ada[i] + b_ada[i]
        sh_a, sc_a, g_a, sh_f, sc_f, g_f = [m[:, None, :] for m in jnp.split(mod, 6, axis=-1)]
        mod_c = jax.nn.silu(c_ctx) @ w_ada[i] + b_ada[i]
        csh_a, csc_a, cg_a, csh_f, csc_f, cg_f = jnp.split(mod_c, 6, axis=-1)

        hc = modulate(ctx, csh_a, csc_a)
        if last:
            pc = None
            kv_c = hc @ w_in[i][:, KV_OFF:POOL_OFF]
        else:
            pc = hc @ w_in[i]
            kv_c = pc[..., KV_OFF:POOL_OFF]
        k_c, v_c = split_kv(kv_c, k_gain[i])

        h = modulate(x, sh_a, sc_a)
        p = h @ w_in[i]
        k_x, v_x = split_kv(p[..., KV_OFF:POOL_OFF], k_gain[i])
        k_all = jnp.concatenate([rope_2d(k_x, tabs), k_c], axis=1)
        v_all = jnp.concatenate([v_x, v_c], axis=1)
        x = x + g_a * mix_tokens(p, k_all, v_all, q_gain[i], w_br_attn[i], w_pool[i],
                                 pool_scale[i], w_out[i], tabs)
        x = x + g_f * peer_ffn(modulate(x, sh_f, sc_f), w_q_peer[i], peer_keys[i],
                               peer_u[i], peer_v[i])

        if not last:
            ctx = ctx + cg_a * mix_tokens(pc, k_c, v_c, q_gain[i], w_br_attn[i], w_pool[i],
                                          pool_scale[i], w_out[i], None)
            ctx = ctx + cg_f * peer_ffn(modulate(ctx, csh_f, csc_f), w_q_peer[i], peer_keys[i],
                                        peer_u[i], peer_v[i])
    return rms(x) * final_gain
```

```python
import functools

import jax
import jax.numpy as jnp
from jax import lax
from jax.experimental import pallas as pl
from jax.experimental.pallas import tpu as pltpu

GRID_W = 64
HEAD_DIM = 128
N_KV_HEADS = 4
ROPE_THETA = 10000.0
POOL_WINDOWS = (2, 4, 8, 16)
PEER_TOPK = 16
EPS = 1e-6

LANES = 128
SUBLANES = 8
VMEM_LIMIT = 60 * 1024 * 1024

F32 = jnp.float32
BF16 = jnp.bfloat16
NEG_INF = float("-inf")


def _tile(n, target):
    t = target
    while t > LANES and n % t:
        t //= 2
    assert n % t == 0, (n, target)
    return t


def _cparams(sem):
    return pltpu.CompilerParams(dimension_semantics=sem, vmem_limit_bytes=VMEM_LIMIT)


def _rms_mod(x, shift, scale):
    r = lax.rsqrt(jnp.mean(x * x, axis=-1, keepdims=True) + EPS)
    return (x * r) * (1.0 + scale) + shift


def _mod_kernel(c_ref, w_ref, b_ref, o_ref):
    c = c_ref[...]
    s = c * jax.nn.sigmoid(c)
    o_ref[...] = jnp.dot(s.astype(BF16), w_ref[...].astype(BF16),
                         preferred_element_type=F32) + b_ref[...]


def _mod(cc, w_ada, b_ada3, layer):
    R, D = cc.shape
    N = w_ada.shape[-1]
    tn = _tile(N, 1024)
    return pl.pallas_call(
        _mod_kernel,
        grid=(N // tn,),
        in_specs=[pl.BlockSpec((R, D), lambda j: (0, 0)),
                  pl.BlockSpec((None, D, tn), lambda j: (layer, 0, j)),
                  pl.BlockSpec((None, 1, tn), lambda j: (layer, 0, j))],
        out_specs=pl.BlockSpec((R, tn), lambda j: (0, j)),
        out_shape=jax.ShapeDtypeStruct((R, N), F32),
        compiler_params=_cparams(("arbitrary",)),
        name="mod",
    )(cc, w_ada, b_ada3)


def _inproj_kernel(x_ref, sh_ref, sc_ref, w_ref, o_ref, h_scr):
    @pl.when(pl.program_id(2) == 0)
    def _():
        h_scr[...] = _rms_mod(x_ref[...], sh_ref[...], sc_ref[...]).astype(BF16)

    o_ref[...] = jnp.dot(h_scr[...], w_ref[...], preferred_element_type=F32)


def _inproj(x, sh, sc, w):
    B, L, D = x.shape
    N = w.shape[1]
    tl = _tile(L, 1024)
    tn = _tile(N, 1024)
    return pl.pallas_call(
        _inproj_kernel,
        grid=(B, L // tl, N // tn),
        in_specs=[pl.BlockSpec((None, tl, D), lambda b, i, j: (b, i, 0)),
                  pl.BlockSpec((None, 1, D), lambda b, i, j: (b, 0, 0)),
                  pl.BlockSpec((None, 1, D), lambda b, i, j: (b, 0, 0)),
                  pl.BlockSpec((D, tn), lambda b, i, j: (0, j))],
        out_specs=pl.BlockSpec((None, tl, tn), lambda b, i, j: (b, i, j)),
        out_shape=jax.ShapeDtypeStruct((B, L, N), F32),
        scratch_shapes=[pltpu.VMEM((tl, D), BF16)],
        compiler_params=_cparams(("arbitrary", "arbitrary", "arbitrary")),
        name="inproj",
    )(x, sh, sc, w)


def _norm_rope(xh, gain, cos, sin, rope):
    r = lax.rsqrt(jnp.mean(xh * xh, axis=-1, keepdims=True) + EPS)
    y = (xh * r) * gain
    if not rope:
        return y
    lane = lax.broadcasted_iota(jnp.int32, y.shape, 1)
    first = (lane % (HEAD_DIM // 2)) < (HEAD_DIM // 4)
    partner = jnp.where(first,
                        pltpu.roll(y, HEAD_DIM - HEAD_DIM // 4, 1),
                        pltpu.roll(y, HEAD_DIM // 4, 1))
    return y * cos + partner * sin


def _prep_kernel(pq_ref, pkv_ref, cos_ref, sin_ref, qg_ref, kg_ref,
                 q_ref, k_ref, v_ref, *, rope, nq, nkv):
    cos = cos_ref[...]
    sin = sin_ref[...]
    qg = qg_ref[...]
    kg = kg_ref[...]
    for h in range(nq):
        sl = slice(h * HEAD_DIM, (h + 1) * HEAD_DIM)
        q_ref[:, sl] = _norm_rope(pq_ref[:, sl], qg, cos, sin, rope).astype(BF16)
    for h in range(nkv):
        sl = slice(h * HEAD_DIM, (h + 1) * HEAD_DIM)
        k_ref[:, sl] = _norm_rope(pkv_ref[:, sl], kg, cos, sin, rope).astype(BF16)
    kvw = nkv * HEAD_DIM
    v_ref[...] = pkv_ref[:, kvw:2 * kvw].astype(BF16)


def _prep(p, cos, sin, qg, kg, q_w, kv_w, rope):
    B, L, _ = p.shape
    tl = _tile(L, 512)
    assert q_w % (2 * kv_w) == 0
    kv_blk = q_w // (2 * kv_w)
    kern = functools.partial(_prep_kernel, rope=rope, nq=q_w // HEAD_DIM, nkv=kv_w // HEAD_DIM)
    return pl.pallas_call(
        kern,
        grid=(B, L // tl),
        in_specs=[pl.BlockSpec((None, tl, q_w), lambda b, i: (b, i, 0)),
                  pl.BlockSpec((None, tl, 2 * kv_w), lambda b, i: (b, i, kv_blk)),
                  pl.BlockSpec((tl, HEAD_DIM), lambda b, i: (i, 0)),
                  pl.BlockSpec((tl, HEAD_DIM), lambda b, i: (i, 0)),
                  pl.BlockSpec((1, HEAD_DIM), lambda b, i: (0, 0)),
                  pl.BlockSpec((1, HEAD_DIM), lambda b, i: (0, 0))],
        out_specs=[pl.BlockSpec((None, tl, q_w), lambda b, i: (b, i, 0)),
                   pl.BlockSpec((None, tl, kv_w), lambda b, i: (b, i, 0)),
                   pl.BlockSpec((None, tl, kv_w), lambda b, i: (b, i, 0))],
        out_shape=[jax.ShapeDtypeStruct((B, L, q_w), BF16),
                   jax.ShapeDtypeStruct((B, L, kv_w), BF16),
                   jax.ShapeDtypeStruct((B, L, kv_w), BF16)],
        compiler_params=_cparams(("arbitrary", "arbitrary")),
        name="prep",
    )(p, p, cos, sin, qg, kg)


def _attn_kernel(q_ref, k_ref, v_ref, o_ref, *, group):
    k = k_ref[...]
    v = v_ref[...]
    scale = HEAD_DIM ** -0.5
    for g in range(group):
        sl = slice(g * HEAD_DIM, (g + 1) * HEAD_DIM)
        s = lax.dot_general(q_ref[:, sl], k, (((1,), (1,)), ((), ())),
                            preferred_element_type=F32) * scale
        m = jnp.max(s, axis=-1, keepdims=True)
        e = jnp.exp(s - m)
        l = jnp.sum(e, axis=-1, keepdims=True)
        o = jnp.dot(e.astype(BF16), v, preferred_element_type=F32) / l
        o_ref[:, sl] = o.astype(BF16)


def _attn(q, k, v):
    B, L, q_w = q.shape
    Lk, kv_w = k.shape[1], k.shape[2]
    nkv = kv_w // HEAD_DIM
    group = q_w // kv_w
    gw = group * HEAD_DIM
    tq = _tile(L, 512)
    return pl.pallas_call(
        functools.partial(_attn_kernel, group=group),
        grid=(B, nkv, L // tq),
        in_specs=[pl.BlockSpec((None, tq, gw), lambda b, h, i: (b, i, h)),
                  pl.BlockSpec((None, Lk, HEAD_DIM), lambda b, h, i: (b, 0, h)),
                  pl.BlockSpec((None, Lk, HEAD_DIM), lambda b, h, i: (b, 0, h))],
        out_specs=pl.BlockSpec((None, tq, gw), lambda b, h, i: (b, i, h)),
        out_shape=jax.ShapeDtypeStruct((B, L, q_w), BF16),
        compiler_params=_cparams(("arbitrary", "arbitrary", "arbitrary")),
        name="attn",
    )(q, k, v)


POOL_HALO = 8


def _pool_kernel(u_ref, w_ref, ps_ref, o_ref, pad_scr, *, L, n_groups):
    u = u_ref[...]
    zeros = jnp.zeros((POOL_HALO, u.shape[1]), F32)
    pad_scr[0:POOL_HALO, :] = zeros
    pad_scr[POOL_HALO + L:2 * POOL_HALO + L, :] = zeros
    pad_scr[POOL_HALO:POOL_HALO + L, :] = u
    t = lax.broadcasted_iota(jnp.int32, (L, 1), 0)
    g = pl.program_id(1)
    for gi in range(n_groups):
        win = POOL_WINDOWS[gi]

        @pl.when(g == gi)
        def _(win=win):
            back = win // 2
            acc = None
            for off in range(-back, win - back):
                piece = pad_scr[POOL_HALO + off:POOL_HALO + off + L, :]
                acc = piece if acc is None else acc + piece
            lo = jnp.maximum(t - back, 0)
            hi = jnp.minimum(t + (win - back), L)
            cnt = (hi - lo).astype(F32)
            pooled = acc / cnt - u
            y = jnp.dot(pooled.astype(BF16), w_ref[...], preferred_element_type=F32)
            o_ref[...] = y * ps_ref[...]


def _pool(p, w_pool, pool_scale, pool_off):
    B, L, _ = p.shape
    G, C, dout = w_pool.shape
    assert G == len(POOL_WINDOWS) and pool_off % C == 0
    blk0 = pool_off // C
    return pl.pallas_call(
        functools.partial(_pool_kernel, L=L, n_groups=G),
        grid=(B, G),
        in_specs=[pl.BlockSpec((None, L, C), lambda b, g: (b, 0, blk0 + g)),
                  pl.BlockSpec((None, C, dout), lambda b, g: (g, 0, 0)),
                  pl.BlockSpec((1, dout), lambda b, g: (0, g))],
        out_specs=pl.BlockSpec((None, L, dout), lambda b, g: (b, 0, g)),
        out_shape=jax.ShapeDtypeStruct((B, L, G * dout), F32),
        scratch_shapes=[pltpu.VMEM((L + 2 * POOL_HALO, C), F32)],
        compiler_params=_cparams(("arbitrary", "arbitrary")),
        name="pool",
    )(p, w_pool, pool_scale)


def _merge_kernel(o_ref, ga_ref, gb_ref, yb_ref, x_ref, gate_ref, wbr_ref, wout_ref, out_ref):
    y_a = jnp.dot(o_ref[...], wbr_ref[...], preferred_element_type=F32)
    merged = jax.nn.sigmoid(ga_ref[...]) * y_a + jax.nn.sigmoid(gb_ref[...]) * yb_ref[...]
    y = jnp.dot(merged.astype(BF16), wout_ref[...], preferred_element_type=F32)
    out_ref[...] = x_ref[...] + gate_ref[...] * y


def _merge(o, p, y_b, x, gate, w_br, w_out, ga_off):
    B, L, D = x.shape
    q_w = o.shape[-1]
    assert ga_off % D == 0
    ga_blk = ga_off // D
    tm = _tile(L, 256)
    tok = lambda b, i: (b, i, 0)
    return pl.pallas_call(
        _merge_kernel,
        grid=(B, L // tm),
        in_specs=[pl.BlockSpec((None, tm, q_w), tok),
                  pl.BlockSpec((None, tm, D), lambda b, i: (b, i, ga_blk)),
                  pl.BlockSpec((None, tm, D), lambda b, i: (b, i, ga_blk + 1)),
                  pl.BlockSpec((None, tm, D), tok),
                  pl.BlockSpec((None, tm, D), tok),
                  pl.BlockSpec((None, 1, D), lambda b, i: (b, 0, 0)),
                  pl.BlockSpec((q_w, D), lambda b, i: (0, 0), pipeline_mode=pl.Buffered(1)),
                  pl.BlockSpec((D, D), lambda b, i: (0, 0), pipeline_mode=pl.Buffered(1))],
        out_specs=pl.BlockSpec((None, tm, D), tok),
        out_shape=jax.ShapeDtypeStruct((B, L, D), F32),
        compiler_params=_cparams(("arbitrary", "arbitrary")),
        name="merge",
    )(o, p, p, y_b, x, gate, w_br, w_out)


N_EXTRACT = PEER_TOPK + 1
SV_ROWS = 24
CAND_Q = 8


def _extract_desc(work, n):
    vals = []
    for _ in range(n):
        mx = jnp.max(work, axis=0, keepdims=True)
        vals.append(mx)
        work = jnp.where(work == mx, NEG_INF, work)
    return vals


def _psel_kernel(x_ref, sh_ref, sc_ref, wq_ref, keys_ref,
                 hf_ref, a_ref, ct_ref, b_ref, s_scr, sv_scr, *, n_heads):
    hf = _rms_mod(x_ref[...], sh_ref[...], sc_ref[...]).astype(BF16)
    hf_ref[...] = hf
    qp = jnp.dot(hf, wq_ref[...], preferred_element_type=F32).astype(BF16)
    n_hp = 2 * n_heads
    for hp in range(n_hp):
        s_scr[hp] = lax.dot_general(keys_ref[hp], qp[:, hp * LANES:(hp + 1) * LANES],
                                    (((1,), (1,)), ((), ())), preferred_element_type=F32)

    tm = s_scr.shape[-1]

    def half_body(hp, carry):
        vals = _extract_desc(s_scr[hp], N_EXTRACT)
        sv_scr[hp, 2 * SUBLANES:3 * SUBLANES, :] = jnp.full((SUBLANES, tm), NEG_INF, F32)
        for r, val in enumerate(vals):
            sv_scr[hp, r:r + 1, :] = val
        return carry

    lax.fori_loop(0, n_hp, half_body, 0)

    def head_body(h, carry):
        sv0 = sv_scr[2 * h]
        sv1 = sv_scr[2 * h + 1]
        m0 = sv0[0:1]
        m1 = sv1[0:1]
        pieces = [m0 + sv1]
        for r in range(1, CAND_Q):
            pieces.append(sv0[r:r + 1] + sv1[0:CAND_Q])
        pieces.append(sv0[CAND_Q:SV_ROWS] + m1)
        cand = jnp.concatenate(pieces, axis=0)
        top = _extract_desc(cand, N_EXTRACT)
        tau = 0.5 * (top[PEER_TOPK - 1] + top[PEER_TOPK])
        m = m0 + m1
        z = jnp.sum(jnp.where(cand >= tau, jnp.exp(cand - m), 0.0), axis=0, keepdims=True)
        s0 = s_scr[2 * h]
        s1 = s_scr[2 * h + 1]
        a_ref[h] = jnp.exp(s0 - m0) / z
        ct_ref[h] = jnp.exp(jnp.minimum((tau - m1) - s0, 80.0))
        b_ref[h] = jnp.exp(s1 - m1)
        return carry

    lax.fori_loop(0, n_heads, head_body, 0)


def _psel(x, sh, sc, w_q, keys):
    B, L, D = x.shape
    n_hp, n_keys, qd = keys.shape
    assert n_keys == LANES and qd == LANES
    n_heads = n_hp // 2
    tm = _tile(L, 256)
    tok = lambda b, i: (b, i, 0)
    sel_spec = pl.BlockSpec((None, n_heads, n_keys, tm), lambda b, i: (b, 0, 0, i))
    sel_shape = jax.ShapeDtypeStruct((B, n_heads, n_keys, L), F32)
    return pl.pallas_call(
        functools.partial(_psel_kernel, n_heads=n_heads),
        grid=(B, L // tm),
        in_specs=[pl.BlockSpec((None, tm, D), tok),
                  pl.BlockSpec((None, 1, D), lambda b, i: (b, 0, 0)),
                  pl.BlockSpec((None, 1, D), lambda b, i: (b, 0, 0)),
                  pl.BlockSpec(w_q.shape, lambda b, i: (0, 0), pipeline_mode=pl.Buffered(1)),
                  pl.BlockSpec(keys.shape, lambda b, i: (0, 0, 0), pipeline_mode=pl.Buffered(1))],
        out_specs=[pl.BlockSpec((None, tm, D), tok), sel_spec, sel_spec, sel_spec],
        out_shape=[jax.ShapeDtypeStruct((B, L, D), BF16), sel_shape, sel_shape, sel_shape],
        scratch_shapes=[pltpu.VMEM((n_hp, n_keys, tm), F32),
                        pltpu.VMEM((n_hp, SV_ROWS, tm), F32)],
        compiler_params=_cparams(("arbitrary", "arbitrary")),
        name="psel",
    )(x, sh, sc, w_q, keys)


PFFN_I_PER_STEP = 8
INV_SQRT2 = 0.7071067811865476


def _pffn_kernel(hf_ref, u_ref, vt_ref, a_ref, ct_ref, b_ref, x_ref, gate_ref, fg_ref,
                 out_ref, at_scr, mt_scr, acc_scr, *, n_heads, final_norm):
    j = pl.program_id(2)
    tm = hf_ref.shape[0]

    at_scr[...] = lax.dot_general(u_ref[...], hf_ref[...], (((1,), (1,)), ((), ())),
                                  preferred_element_type=F32)

    for ii in range(PFFN_I_PER_STEP):
        rs = slice(ii * LANES, (ii + 1) * LANES)
        for lc in range(tm // LANES):
            ls = slice(lc * LANES, (lc + 1) * LANES)
            w = jnp.zeros((LANES, LANES), F32)
            for h in range(n_heads):
                a = a_ref[h, ii:ii + 1, ls]
                ct = ct_ref[h, ii:ii + 1, ls]
                b = b_ref[h, :, ls]
                w = w + a * jnp.where(b >= ct, b, 0.0)
            pre = at_scr[rs, ls]
            act = 0.5 * pre * (1.0 + lax.erf(pre * INV_SQRT2))
            mt_scr[rs, ls] = (w * act).astype(BF16)

    contrib = jnp.dot(vt_ref[...], mt_scr[...], preferred_element_type=F32)

    @pl.when(j == 0)
    def _():
        acc_scr[...] = contrib

    @pl.when(j > 0)
    def _():
        acc_scr[...] += contrib

    @pl.when(j == pl.num_programs(2) - 1)
    def _():
        D = acc_scr.shape[0]
        cw = min(512, D)
        for c in range(D // cw):
            cs = slice(c * cw, (c + 1) * cw)
            out_ref[:, cs] = x_ref[:, cs] + gate_ref[:, cs] * acc_scr[cs, :].T
        if final_norm:
            y = out_ref[...]
            r = lax.rsqrt(jnp.mean(y * y, axis=-1, keepdims=True) + EPS)
            out_ref[...] = (y * r) * fg_ref[...]


def _pffn(hf, u, vt, a_t, ct_t, b_t, x, gate, final_gain, final_norm):
    B, L, D = x.shape
    E = u.shape[0]
    n_heads, n_keys = a_t.shape[1], a_t.shape[2]
    be = PFFN_I_PER_STEP * n_keys
    assert E == n_keys * n_keys and E % be == 0
    tm = _tile(L, 512)
    tok = lambda b, i, j: (b, i, 0)
    return pl.pallas_call(
        functools.partial(_pffn_kernel, n_heads=n_heads, final_norm=final_norm),
        grid=(B, L // tm, E // be),
        in_specs=[pl.BlockSpec((None, tm, D), tok),
                  pl.BlockSpec((be, D), lambda b, i, j: (j, 0)),
                  pl.BlockSpec((D, be), lambda b, i, j: (0, j)),
                  pl.BlockSpec((None, n_heads, PFFN_I_PER_STEP, tm), lambda b, i, j: (b, 0, j, i)),
                  pl.BlockSpec((None, n_heads, PFFN_I_PER_STEP, tm), lambda b, i, j: (b, 0, j, i)),
                  pl.BlockSpec((None, n_heads, n_keys, tm), lambda b, i, j: (b, 0, 0, i)),
                  pl.BlockSpec((None, tm, D), tok),
                  pl.BlockSpec((None, 1, D), lambda b, i, j: (b, 0, 0)),
                  pl.BlockSpec((1, D), lambda b, i, j: (0, 0))],
        out_specs=pl.BlockSpec((None, tm, D), tok),
        out_shape=jax.ShapeDtypeStruct((B, L, D), F32),
        scratch_shapes=[pltpu.VMEM((be, tm), F32),
                        pltpu.VMEM((be, tm), BF16),
                        pltpu.VMEM((D, tm), F32)],
        compiler_params=_cparams(("arbitrary", "arbitrary", "arbitrary")),
        name="pffn",
    )(hf, u, vt, a_t, ct_t, b_t, x, gate, final_gain)


def _rope_tables(L):
    n_freq = HEAD_DIM // 4
    t = jnp.arange(L)
    row = (t // GRID_W).astype(F32)[:, None]
    col = (t % GRID_W).astype(F32)[:, None]
    inv = ROPE_THETA ** (-jnp.arange(n_freq, dtype=F32) / n_freq)
    ar, ac = row * inv, col * inv
    cos = jnp.concatenate([jnp.cos(ar), jnp.cos(ar), jnp.cos(ac), jnp.cos(ac)], axis=-1)
    sin = jnp.concatenate([-jnp.sin(ar), jnp.sin(ar), -jnp.sin(ac), jnp.sin(ac)], axis=-1)
    return cos, sin


def kernel(x, c, ctx, c_ctx, w_ada, b_ada, w_in, q_gain, k_gain, w_br_attn, w_pool,
           pool_scale, w_out, w_q_peer, peer_keys, peer_u, peer_v, final_gain):
    B, L, D = x.shape
    depth = w_ada.shape[0]
    q_w = w_br_attn.shape[1]
    pool_w = w_pool.shape[1] * w_pool.shape[2]
    in_w = w_in.shape[-1]
    kv_w = (in_w - q_w - pool_w - 2 * D) // 2
    assert kv_w == N_KV_HEADS * HEAD_DIM
    pool_off = q_w + 2 * kv_w
    ga_off = pool_off + pool_w
    n_heads = peer_keys.shape[1]

    cos, sin = _rope_tables(L)
    ones = jnp.ones_like(cos)
    mod_rows = 16
    assert B + 1 <= mod_rows
    cc = jnp.zeros((mod_rows, D), F32).at[:B].set(c).at[B].set(c_ctx)
    b_ada3 = b_ada[:, None, :]
    fg = final_gain[None, :]

    def mixer(xs, p, k, v, gate, i, rope_q, q):
        o = _attn(q, k, v)
        y_b = _pool(p, w_pool[i].astype(BF16), pool_scale[i][None, :], pool_off)
        return _merge(o, p, y_b, xs, gate, w_br_attn[i].astype(BF16), w_out[i].astype(BF16), ga_off)

    def ffn(xs, sh, sc, gate, i, final_norm):
        keys = peer_keys[i].reshape(2 * n_heads, peer_keys.shape[3], peer_keys.shape[4]).astype(BF16)
        hf, a_t, ct_t, b_t = _psel(xs, sh, sc, w_q_peer[i].astype(BF16), keys)
        return _pffn(hf, peer_u[i].astype(BF16), peer_v[i].T.astype(BF16), a_t, ct_t, b_t,
                     xs, gate, fg, final_norm)

    for i in range(depth):
        last = i == depth - 1
        mod = _mod(cc, w_ada, b_ada3, i)
        lat = [mod[:B, None, k * D:(k + 1) * D] for k in range(6)]
        con = [jnp.broadcast_to(mod[B, k * D:(k + 1) * D][None, None, :], (B, 1, D)) for k in range(6)]
        w_in_b = w_in[i].astype(BF16)
        qg = q_gain[i][None, :]
        kg = k_gain[i][None, :]

        pc = _inproj(ctx, con[0], con[1], w_in_b)
        q_c, k_c, v_c = _prep(pc, ones, ones, qg, kg, q_w, kv_w, rope=False)

        p = _inproj(x, lat[0], lat[1], w_in_b)
        q_x, k_x, v_x = _prep(p, cos, sin, qg, kg, q_w, kv_w, rope=True)
        k_all = jnp.concatenate([k_x, k_c], axis=1)
        v_all = jnp.concatenate([v_x, v_c], axis=1)
        x = mixer(x, p, k_all, v_all, lat[2], i, True, q_x)
        x = ffn(x, lat[3], lat[4], lat[5], i, last)

        if not last:
            ctx = mixer(ctx, pc, k_c, v_c, con[2], i, False, q_c)
            ctx = ffn(ctx, con[3], con[4], con[5], i, False)
    return x
```
